```python
import jax, jax.numpy as jnp
from jax import lax
import numpy as np

D_MODEL = 4096
BATCH = 2
SEQ = 4096
DEPTH = 2
DEC_BATCH = 4
DEC_SEQ = 2048
PAST_LEN = 128

ATT_HEAD_DIM = 128
ATT_HEADS = D_MODEL // 2 // ATT_HEAD_DIM
ATT_WIDTH = ATT_HEADS * ATT_HEAD_DIM
DIL_PATTERNS = ((128, 1), (512, 4), (2048, 16))
ATT_QBLOCK = 64
RET_V_DIM = 256
RET_QK_DIM = RET_V_DIM // 2
RET_HEADS = D_MODEL // 2 // RET_V_DIM
RET_QK_WIDTH = RET_HEADS * RET_QK_DIM
RET_WIDTH = RET_HEADS * RET_V_DIM
RET_CHUNK = 128
ROPE_BASE = 10000.0
IN_WIDTHS = (RET_QK_WIDTH, RET_QK_WIDTH, RET_WIDTH, RET_WIDTH, ATT_WIDTH, ATT_WIDTH, ATT_WIDTH)
IN_COLS = 2 * RET_QK_WIDTH + 2 * RET_WIDTH + 3 * ATT_WIDTH
MIX_WIDTH = RET_WIDTH + ATT_WIDTH
T5_BUCKETS = 32
T5_MAX_DISTANCE = 1024
CROSS_HEADS = 4
CROSS_HEAD_DIM = 128
CROSS_WIDTH = CROSS_HEADS * CROSS_HEAD_DIM
MEM_TOKENS = 256
D_FF = 11008
N_EXPERTS = 8
TOP_K = 2
D_FF_EXPERT = 14336
MOE_BLOCK = 256
N_DENSE_LAYERS = (DEPTH + 1) // 2
N_MOE_LAYERS = DEPTH // 2
NORM_EPS = 1e-6
NEG_INF = -1e30

kernel_name = 'hymba_retention_dilated_moe_encoder'


def rmsnorm(x, g):
    xf = x.astype(jnp.float32)
    y = xf * lax.rsqrt(jnp.mean(xf * xf, axis=-1, keepdims=True) + NORM_EPS)
    return (y * g.astype(jnp.float32)).astype(x.dtype)


def rotary(x):
    S, d = x.shape[1], x.shape[-1]
    half = d // 2
    inv = ROPE_BASE ** (-jnp.arange(half, dtype=jnp.float32) / half)
    ang = jnp.arange(S, dtype=jnp.float32)[:, None] * inv[None, :]
    cos = jnp.cos(ang)[None, :, None, :]
    sin = jnp.sin(ang)[None, :, None, :]
    x1, x2 = x[..., :half], x[..., half:]
    return jnp.concatenate([x1 * cos - x2 * sin, x1 * sin + x2 * cos], axis=-1)


def retention_log_decays():
    h = np.arange(RET_HEADS, dtype=np.float32)
    fwd = np.log(1.0 - 2.0 ** (-5.0 - h)).astype(np.float32)
    bwd = np.log(1.0 - 2.0 ** (-5.5 - h)).astype(np.float32)
    return jnp.asarray(fwd), jnp.asarray(bwd)


def retention_scan(q, k, v, log_gamma, include_diag):
    B, S, H, dk = q.shape
    dv = v.shape[-1]
    C = RET_CHUNK
    n = S // C
    pos = jnp.arange(C, dtype=jnp.float32)
    diff = pos[:, None] - pos[None, :]
    keep = (diff >= 0) if include_diag else (diff > 0)
    dmask = jnp.where(keep[None], jnp.exp(jnp.maximum(diff, 0.0)[None] * log_gamma[:, None, None]), 0.0)
    q_dec = jnp.exp((pos[:, None] + 1.0) * log_gamma[None, :])
    k_dec = jnp.exp((C - 1.0 - pos)[:, None] * log_gamma[None, :])
    c_dec = jnp.exp(C * log_gamma)

    def chunks(t):
        return jnp.swapaxes(t.reshape(B, n, C, H, t.shape[-1]), 0, 1)

    def step(state, inp):
        qc, kc, vc = inp
        s = jnp.einsum('bnhd,bmhd->bhnm', qc, kc) * dmask[None]
        inner = jnp.einsum('bhnm,bmhv->bnhv', s, vc)
        cross = jnp.einsum('bnhd,bhdv->bnhv', qc, state) * q_dec[None, :, :, None]
        state = state * c_dec[None, :, None, None] + jnp.einsum('bmhd,bmhv->bhdv', kc * k_dec[None, :, :, None], vc)
        return state, inner + cross

    init = jnp.zeros((B, H, dk, dv), jnp.float32)
    _, out = lax.scan(step, init, (chunks(q), chunks(k), chunks(v)))
    return jnp.swapaxes(out, 0, 1).reshape(B, S, H, dv)


def retention(q, k, v, g):
    B, S = q.shape[0], q.shape[1]
    lg_fwd, lg_bwd = retention_log_decays()
    qf = rotary(q.astype(jnp.float32)) * RET_QK_DIM ** -0.5
    kf = rotary(k.astype(jnp.float32))
    vf = v.astype(jnp.float32)
    fwd = retention_scan(qf, kf, vf, lg_fwd, True)
    flip = lambda t: jnp.flip(t, axis=1)
    bwd = flip(retention_scan(flip(qf), flip(kf), flip(vf), lg_bwd, False))
    o = fwd + bwd
    mu = jnp.mean(o, axis=-1, keepdims=True)
    var = jnp.mean(jnp.square(o - mu), axis=-1, keepdims=True)
    o = (o - mu) * lax.rsqrt(var + NORM_EPS)
    o = o.reshape(B, S, RET_WIDTH) * jax.nn.silu(g.astype(jnp.float32))
    return o.astype(g.dtype)


def t5_bucket(rel):
    nb = T5_BUCKETS // 2
    max_exact = nb // 2
    side = (rel > 0).astype(np.int32) * nb
    n = np.abs(rel)
    large = max_exact + (np.log(np.maximum(n, 1) / max_exact) / np.log(T5_MAX_DISTANCE / max_exact) * (nb - max_exact)).astype(np.int32)
    large = np.minimum(large, nb - 1)
    return side + np.where(n < max_exact, n, large)


def dilated_attention(q, k, v, rel_bias):
    B, S, H, dh = q.shape
    dtype = q.dtype
    qf = q.astype(jnp.float32) * dh ** -0.5
    kf = k.astype(jnp.float32)
    vf = v.astype(jnp.float32)
    qi = np.arange(ATT_QBLOCK)
    pats = []
    for window, dil in DIL_PATTERNS:
        n_side = window // 2 // dil
        offs = np.arange(-n_side, n_side + 1) * dil
        pad = n_side * dil
        kp = jnp.pad(kf, ((0, 0), (pad, pad), (0, 0), (0, 0)))
        vp = jnp.pad(vf, ((0, 0), (pad, pad), (0, 0), (0, 0)))
        gidx = qi[:, None] + pad + offs[None, :]
        rpos = qi[:, None] + offs[None, :]
        bias = rel_bias[t5_bucket(offs)].astype(jnp.float32).T
        pats.append((kp, vp, gidx, rpos, bias, ATT_QBLOCK + 2 * pad))

    def one_block(b):
        s0 = b * ATT_QBLOCK
        qb = lax.dynamic_slice_in_dim(qf, s0, ATT_QBLOCK, axis=1)
        outs, lses = [], []
        for kp, vp, gidx, rpos, bias, slab in pats:
            kg = lax.dynamic_slice_in_dim(kp, s0, slab, axis=1)[:, gidx]
            vg = lax.dynamic_slice_in_dim(vp, s0, slab, axis=1)[:, gidx]
            pos = s0 + jnp.asarray(rpos)
            valid = (pos >= 0) & (pos < S)
            logits = jnp.einsum('bqhd,bqjhd->bhqj', qb, kg) + bias[None, :, None, :]
            logits = jnp.where(valid[None, None], logits, NEG_INF)
            m = jnp.max(logits, axis=-1, keepdims=True)
            p = jnp.exp(logits - m)
            den = jnp.sum(p, axis=-1)
            o = jnp.einsum('bhqj,bqjhd->bqhd', p, vg) / jnp.swapaxes(den, 1, 2)[..., None]
            outs.append(o)
            lses.append(m[..., 0] + jnp.log(den))
        w = jax.nn.softmax(jnp.stack(lses), axis=0)
        return jnp.einsum('pbhq,pbqhd->bqhd', w, jnp.stack(outs))

    ob = lax.map(one_block, jnp.arange(S // ATT_QBLOCK))
    return jnp.swapaxes(ob, 0, 1).reshape(B, S, H * dh).astype(dtype)


def memory_cross_attention(h, m, w_q, w_kv, w_o):
    B, S, _ = h.shape
    M = m.shape[1]
    q = (h @ w_q).reshape(B, S, CROSS_HEADS, CROSS_HEAD_DIM)
    kv = (m @ w_kv).reshape(B, M, 2, CROSS_HEADS, CROSS_HEAD_DIM)
    k, v = kv[:, :, 0], kv[:, :, 1]
    s = jnp.einsum('bshd,bmhd->bhsm', q, k, preferred_element_type=jnp.float32) * CROSS_HEAD_DIM ** -0.5
    p = jax.nn.softmax(s, axis=-1)
    o = jnp.einsum('bhsm,bmhd->bshd', p.astype(v.dtype), v).reshape(B, S, CROSS_WIDTH)
    return o @ w_o


def swiglu(h, w_gate, w_up, w_down):
    return (jax.nn.silu(h @ w_gate) * (h @ w_up)) @ w_down


def moe_swiglu(h, w_router, w_g, w_u, w_d):
    B, S, D = h.shape
    N = B * S
    A = N * TOP_K
    xt = h.reshape(N, D)
    logits = jnp.dot(xt, w_router, preferred_element_type=jnp.float32)
    top_val, top_idx = lax.top_k(logits, TOP_K)
    gates = jax.nn.softmax(top_val, axis=-1)
    flat_e = top_idx.reshape(A)
    order = jnp.argsort(flat_e)
    e_sorted = flat_e[order]
    tok_sorted = (order // TOP_K).astype(jnp.int32)
    gate_sorted = gates.reshape(A)[order]
    counts = jnp.bincount(flat_e, length=N_EXPERTS)
    padded = (counts + MOE_BLOCK - 1) // MOE_BLOCK * MOE_BLOCK
    pad_end = jnp.cumsum(padded)
    grp_start = jnp.cumsum(counts) - counts
    slot = (pad_end - padded)[e_sorted] + jnp.arange(A) - grp_start[e_sorted]
    n_blocks = -(-A // MOE_BLOCK) + N_EXPERTS
    n_slots = n_blocks * MOE_BLOCK
    slot_tok = jnp.zeros((n_slots,), jnp.int32).at[slot].set(tok_sorted)
    slot_gate = jnp.zeros((n_slots,), jnp.float32).at[slot].set(gate_sorted)
    starts = jnp.arange(n_blocks) * MOE_BLOCK
    block_expert = jnp.minimum(jnp.sum(starts[:, None] >= pad_end[None, :], axis=1), N_EXPERTS - 1)
    xs = xt[slot_tok].reshape(n_blocks, MOE_BLOCK, D)

    def expert_block(args):
        xb, e = args
        return (jax.nn.silu(xb @ w_g[e]) * (xb @ w_u[e])) @ w_d[e]

    ys = lax.map(expert_block, (xs, block_expert)).reshape(n_slots, D)
    y = jnp.zeros_like(xt).at[slot_tok].add(ys * slot_gate[:, None].astype(ys.dtype))
    return y.reshape(B, S, D)


def encoder_trunk(x, mem, rel_bias, g_mix, w_in, w_out, g_cross, g_mem, w_cq, w_ckv, w_co,
                  g_ffn, w_ff_gate, w_ff_up, w_ff_down, w_router, w_exp_gate, w_exp_up, w_exp_down, g_final):
    B, S, _ = x.shape
    split_at = np.cumsum(IN_WIDTHS)[:-1].tolist()
    for l in range(DEPTH):
        h = rmsnorm(x, g_mix[l])
        proj = h @ w_in[l]
        rq, rk, rv, rg, aq, ak, av = jnp.split(proj, split_at, axis=-1)
        ret = retention(rq.reshape(B, S, RET_HEADS, RET_QK_DIM),
                        rk.reshape(B, S, RET_HEADS, RET_QK_DIM),
                        rv.reshape(B, S, RET_HEADS, RET_V_DIM), rg)
        att = dilated_attention(aq.reshape(B, S, ATT_HEADS, ATT_HEAD_DIM),
                                ak.reshape(B, S, ATT_HEADS, ATT_HEAD_DIM),
                                av.reshape(B, S, ATT_HEADS, ATT_HEAD_DIM), rel_bias)
        x = x + jnp.concatenate([ret, att], axis=-1) @ w_out[l]
        x = x + memory_cross_attention(rmsnorm(x, g_cross[l]), rmsnorm(mem, g_mem[l]),
                                       w_cq[l], w_ckv[l], w_co[l])
        h = rmsnorm(x, g_ffn[l])
        if l % 2 == 0:
            i = l // 2
            x = x + swiglu(h, w_ff_gate[i], w_ff_up[i], w_ff_down[i])
        else:
            i = l // 2
            x = x + moe_swiglu(h, w_router[i], w_exp_gate[i], w_exp_up[i], w_exp_down[i])
    return rmsnorm(x, g_final)


def setup_inputs(seed: int = 0) -> dict:
    key = jax.random.key(seed)
    k = jax.random.split(key, 22)
    f32 = jnp.float32

    def nrm(i, shape, scale):
        return jax.random.normal(k[i], shape, f32) * scale

    def gain(i, shape):
        return 1.0 + nrm(i, shape, 0.02)

    return {
        'x_prompt': nrm(0, (BATCH, SEQ, D_MODEL), 1.0),
        'x_sample': nrm(1, (DEC_BATCH, DEC_SEQ, D_MODEL), 1.0),
        'mem_prompt': nrm(2, (BATCH, MEM_TOKENS, D_MODEL), 1.0),
        'mem_sample': nrm(3, (DEC_BATCH, MEM_TOKENS, D_MODEL), 1.0),
        'rel_bias': nrm(4, (T5_BUCKETS, ATT_HEADS), 0.5),
        'g_mix': gain(5, (DEPTH, D_MODEL)),
        'w_in': nrm(6, (DEPTH, D_MODEL, IN_COLS), D_MODEL ** -0.5),
        'w_out': nrm(7, (DEPTH, MIX_WIDTH, D_MODEL), MIX_WIDTH ** -0.5),
        'g_cross': gain(8, (DEPTH, D_MODEL)),
        'g_mem': gain(9, (DEPTH, D_MODEL)),
        'w_cq': nrm(10, (DEPTH, D_MODEL, CROSS_WIDTH), D_MODEL ** -0.5),
        'w_ckv': nrm(11, (DEPTH, D_MODEL, 2 * CROSS_WIDTH), D_MODEL ** -0.5),
        'w_co': nrm(12, (DEPTH, CROSS_WIDTH, D_MODEL), CROSS_WIDTH ** -0.5),
        'g_ffn': gain(13, (DEPTH, D_MODEL)),
        'w_ff_gate': nrm(14, (N_DENSE_LAYERS, D_MODEL, D_FF), D_MODEL ** -0.5),
        'w_ff_up': nrm(15, (N_DENSE_LAYERS, D_MODEL, D_FF), D_MODEL ** -0.5),
        'w_ff_down': nrm(16, (N_DENSE_LAYERS, D_FF, D_MODEL), D_FF ** -0.5),
        'w_router': nrm(17, (N_MOE_LAYERS, D_MODEL, N_EXPERTS), D_MODEL ** -0.5),
        'w_exp_gate': nrm(18, (N_MOE_LAYERS, N_EXPERTS, D_MODEL, D_FF_EXPERT), D_MODEL ** -0.5),
        'w_exp_up': nrm(19, (N_MOE_LAYERS, N_EXPERTS, D_MODEL, D_FF_EXPERT), D_MODEL ** -0.5),
        'w_exp_down': nrm(20, (N_MOE_LAYERS, N_EXPERTS, D_FF_EXPERT, D_MODEL), D_FF_EXPERT ** -0.5),
        'g_final': gain(21, (D_MODEL,)),
    }


def reference(x_prompt, x_sample, mem_prompt, mem_sample, rel_bias, g_mix, w_in, w_out, g_cross, g_mem,
              w_cq, w_ckv, w_co, g_ffn, w_ff_gate, w_ff_up, w_ff_down, w_router, w_exp_gate, w_exp_up,
              w_exp_down, g_final):
    y_prompt = encoder_trunk(x_prompt, mem_prompt, rel_bias, g_mix, w_in, w_out, g_cross, g_mem, w_cq, w_ckv, w_co,
                             g_ffn, w_ff_gate, w_ff_up, w_ff_down, w_router, w_exp_gate, w_exp_up, w_exp_down, g_final)
    y_sample = encoder_trunk(x_sample, mem_sample, rel_bias, g_mix, w_in, w_out, g_cross, g_mem, w_cq, w_ckv, w_co,
                             g_ffn, w_ff_gate, w_ff_up, w_ff_down, w_router, w_exp_gate, w_exp_up, w_exp_down, g_final)
    return (y_prompt, y_sample)
```

```python
import functools

import numpy as np
import jax
import jax.numpy as jnp
from jax import lax
from jax.experimental import pallas as pl
from jax.experimental.pallas import tpu as pltpu

F32 = jnp.float32
BF16 = jnp.bfloat16

D_MODEL = 4096
DEPTH = 2
ATT_HEAD_DIM = 128
ATT_HEADS = 16
ATT_WIDTH = ATT_HEADS * ATT_HEAD_DIM
DIL_PATTERNS = ((128, 1), (512, 4), (2048, 16))
ATT_SIDE = 64
RET_V_DIM = 256
RET_QK_DIM = 128
RET_HEADS = 8
RET_QK_WIDTH = RET_HEADS * RET_QK_DIM
RET_WIDTH = RET_HEADS * RET_V_DIM
ROPE_BASE = 10000.0
IN_COLS = 2 * RET_QK_WIDTH + 2 * RET_WIDTH + 3 * ATT_WIDTH
COL_RQ, COL_RK, COL_RV, COL_RG = 0, RET_QK_WIDTH, 2 * RET_QK_WIDTH, 2 * RET_QK_WIDTH + RET_WIDTH
COL_AQ = COL_RG + RET_WIDTH
COL_AK = COL_AQ + ATT_WIDTH
COL_AV = COL_AK + ATT_WIDTH
T5_BUCKETS = 32
T5_MAX_DISTANCE = 1024
CROSS_HEADS = 4
CROSS_HEAD_DIM = 128
CROSS_WIDTH = CROSS_HEADS * CROSS_HEAD_DIM
N_EXPERTS = 8
TOP_K = 2
NORM_EPS = 1e-6
NEG_INF = -1e30

LANES = 128
VMEM_LIMIT = 56 * 1024 * 1024

RET_CHUNK = 256
ATT_QBLOCK = 128
ATT_PAD = ATT_SIDE * DIL_PATTERNS[-1][1]
MOE_TM = 512
FF_PAD = 512


def _params(sem):
    return pltpu.CompilerParams(dimension_semantics=sem, vmem_limit_bytes=VMEM_LIMIT)


def _rms(x, g):
    ms = jnp.mean(x * x, axis=-1, keepdims=True)
    return x * lax.rsqrt(ms + NORM_EPS) * g


def _rmsnorm_kernel(x_ref, g_ref, o_ref):
    o_ref[...] = _rms(x_ref[...], g_ref[...]).astype(o_ref.dtype)


def rmsnorm(x, g, out_dtype, tm=512):
    M, D = x.shape
    assert M % tm == 0, (M, tm)
    return pl.pallas_call(
        _rmsnorm_kernel,
        grid=(M // tm,),
        in_specs=[pl.BlockSpec((tm, D), lambda i: (i, 0)),
                  pl.BlockSpec((1, D), lambda i: (0, 0))],
        out_specs=pl.BlockSpec((tm, D), lambda i: (i, 0)),
        out_shape=jax.ShapeDtypeStruct((M, D), out_dtype),
        compiler_params=_params(("parallel",)),
        name="rmsnorm",
    )(x, g.reshape(1, D))


def _mm_kernel(a_ref, b_ref, *rest, has_res):
    acc = jnp.dot(a_ref[...], b_ref[...], preferred_element_type=F32)
    if has_res:
        acc = acc + rest[0][...]
    rest[-1][...] = acc.astype(rest[-1].dtype)


def matmul(a, w, res=None, out_dtype=F32, tm=1024, tn=512):
    M, K = a.shape
    N = w.shape[1]
    tm, tn = min(tm, M), min(tn, N)
    assert M % tm == 0 and N % tn == 0, (M, tm, N, tn)
    in_specs = [pl.BlockSpec((tm, K), lambda i, j: (i, 0)),
                pl.BlockSpec((K, tn), lambda i, j: (0, j))]
    args = [a, w]
    if res is not None:
        in_specs.append(pl.BlockSpec((tm, tn), lambda i, j: (i, j)))
        args.append(res)
    return pl.pallas_call(
        functools.partial(_mm_kernel, has_res=res is not None),
        grid=(M // tm, N // tn),
        in_specs=in_specs,
        out_specs=pl.BlockSpec((tm, tn), lambda i, j: (i, j)),
        out_shape=jax.ShapeDtypeStruct((M, N), out_dtype),
        compiler_params=_params(("parallel", "parallel")),
        name="matmul",
    )(*args)


def _rotate(x, cos, sin):
    return x * cos + pltpu.roll(x, RET_QK_DIM // 2, 1) * sin


def _retention_kernel(q_ref, k_ref, v_ref, g_ref, cos_ref, sin_ref, dm_ref, dec_ref, cd_ref,
                      o_ref, qr_s, kr_s, acc_s, st_s, *, S, C):
    n = S // C
    scale = RET_QK_DIM ** -0.5
    nt_dims = (((1,), (1,)), ((), ()))

    def kv_outer(kd, v):
        return jnp.dot(kd.T.astype(BF16), v, preferred_element_type=F32)

    st_s[...] = jnp.zeros_like(st_s)

    def fwd(c, carry):
        rows = pl.ds(pl.multiple_of(c * C, C), C)
        cos, sin = cos_ref[rows, :], sin_ref[rows, :]
        qr = _rotate(q_ref[rows, :], cos, sin) * scale
        kr = _rotate(k_ref[rows, :], cos, sin)
        qr_s[rows, :] = qr
        kr_s[rows, :] = kr
        v = v_ref[rows, :].astype(BF16)
        s = lax.dot_general(qr.astype(BF16), kr.astype(BF16), nt_dims,
                            preferred_element_type=F32) * dm_ref[0]
        inner = jnp.dot(s.astype(BF16), v, preferred_element_type=F32)
        cross = jnp.dot((qr * dec_ref[0, 0]).astype(BF16), st_s[...].astype(BF16),
                        preferred_element_type=F32)
        acc_s[rows, :] = inner + cross
        st_s[...] = st_s[...] * cd_ref[0, 0:1, :] + kv_outer(kr * dec_ref[0, 1], v)
        return carry

    lax.fori_loop(0, n, fwd, 0)

    st_s[...] = jnp.zeros_like(st_s)

    def bwd(i, carry):
        c = n - 1 - i
        rows = pl.ds(pl.multiple_of(c * C, C), C)
        qr, kr = qr_s[rows, :], kr_s[rows, :]
        v = v_ref[rows, :].astype(BF16)
        cross = jnp.dot((qr * dec_ref[0, 2]).astype(BF16), st_s[...].astype(BF16),
                        preferred_element_type=F32)
        o = acc_s[rows, :] + cross
        mu = jnp.mean(o, axis=-1, keepdims=True)
        d = o - mu
        var = jnp.mean(d * d, axis=-1, keepdims=True)
        g = g_ref[rows, :]
        o_ref[rows, :] = (d * lax.rsqrt(var + NORM_EPS) * (g * jax.nn.sigmoid(g))).astype(o_ref.dtype)
        st_s[...] = st_s[...] * cd_ref[0, 1:2, :] + kv_outer(kr * dec_ref[0, 3], v)
        return carry

    lax.fori_loop(0, n, bwd, 0)


def _retention_tables(C):
    h = np.arange(RET_HEADS, dtype=np.float32)
    lg_f = np.log(1.0 - 2.0 ** (-5.0 - h)).astype(np.float32)
    lg_b = np.log(1.0 - 2.0 ** (-5.5 - h)).astype(np.float32)
    pos = np.arange(C, dtype=np.float32)
    diff = pos[:, None] - pos[None, :]
    m_f = np.where(diff >= 0, np.exp(np.maximum(diff, 0.0)[None] * lg_f[:, None, None]), 0.0)
    m_b = np.where(diff < 0, np.exp(np.maximum(-diff, 0.0)[None] * lg_b[:, None, None]), 0.0)
    dmask = (m_f + m_b).astype(np.float32)
    q_f = np.exp((pos[None, :] + 1.0) * lg_f[:, None])
    k_f = np.exp((C - 1.0 - pos)[None, :] * lg_f[:, None])
    q_b = np.exp((C - pos)[None, :] * lg_b[:, None])
    k_b = np.exp(pos[None, :] * lg_b[:, None])
    dec = np.stack([q_f, k_f, q_b, k_b], axis=1).astype(np.float32)
    dec = np.broadcast_to(dec[..., None], (RET_HEADS, 4, C, RET_QK_DIM)).copy()
    cd = np.zeros((RET_HEADS, 8, RET_V_DIM), np.float32)
    cd[:, 0, :] = np.exp(C * lg_f)[:, None]
    cd[:, 1, :] = np.exp(C * lg_b)[:, None]
    return jnp.asarray(dmask), jnp.asarray(dec), jnp.asarray(cd)


def _rope_tables(S):
    half = RET_QK_DIM // 2
    inv = ROPE_BASE ** (-jnp.arange(half, dtype=F32) / half)
    ang = jnp.arange(S, dtype=F32)[:, None] * inv[None, :]
    cos, sin = jnp.cos(ang), jnp.sin(ang)
    return jnp.concatenate([cos, cos], axis=1), jnp.concatenate([-sin, sin], axis=1)


def retention(proj, row_start, S, nseq, C=RET_CHUNK):
    C = min(C, S)
    rb = row_start // S
    cosf, sinf = _rope_tables(S)
    dmask, dec, cd = _retention_tables(C)
    qk, dv = RET_QK_DIM, RET_V_DIM
    return pl.pallas_call(
        functools.partial(_retention_kernel, S=S, C=C),
        grid=(nseq, RET_HEADS),
        in_specs=[
            pl.BlockSpec((S, qk), lambda b, h: (rb + b, COL_RQ // qk + h)),
            pl.BlockSpec((S, qk), lambda b, h: (rb + b, COL_RK // qk + h)),
            pl.BlockSpec((S, dv), lambda b, h: (rb + b, COL_RV // dv + h)),
            pl.BlockSpec((S, dv), lambda b, h: (rb + b, COL_RG // dv + h)),
            pl.BlockSpec((S, qk), lambda b, h: (0, 0)),
            pl.BlockSpec((S, qk), lambda b, h: (0, 0)),
            pl.BlockSpec((1, C, C), lambda b, h: (h, 0, 0)),
            pl.BlockSpec((1, 4, C, qk), lambda b, h: (h, 0, 0, 0)),
            pl.BlockSpec((1, 8, dv), lambda b, h: (h, 0, 0)),
        ],
        out_specs=pl.BlockSpec((S, dv), lambda b, h: (b, h)),
        out_shape=jax.ShapeDtypeStruct((nseq * S, RET_WIDTH), BF16),
        scratch_shapes=[pltpu.VMEM((S, qk), F32), pltpu.VMEM((S, qk), F32),
                        pltpu.VMEM((S, dv), F32), pltpu.VMEM((qk, dv), F32)],
        compiler_params=_params(("parallel", "parallel")),
        name="retention",
    )(proj, proj, proj, proj, cosf, sinf, dmask, dec, cd)


def _dilattn_kernel(q_ref, k_ref, v_ref, tb_ref, o_ref, kp_s, vp_s, acc_s, m_s, l_s, *, S, QB):
    P = ATT_PAD
    W = QB + 2 * ATT_SIDE
    scale = ATT_HEAD_DIM ** -0.5
    nt_dims = (((1,), (1,)), ((), ()))

    zpad = jnp.zeros((P, ATT_HEAD_DIM), F32)
    for buf, src in ((kp_s, k_ref), (vp_s, v_ref)):
        buf[0:P, :] = zpad
        buf[P + S:P + S + P, :] = zpad
        buf[P:P + S, :] = src[...]

    col = lax.broadcasted_iota(jnp.int32, (QB, W), 1)

    for p, (_, d) in enumerate(DIL_PATTERNS):
        Sd = S // d
        nb = Sd // QB
        first, last = p == 0, p == len(DIL_PATTERNS) - 1

        def rows_of(start, size, d=d):
            return pl.ds(start, size) if d == 1 else pl.ds(start, size, stride=d)

        def block(t, carry, p=p, d=d, Sd=Sd, nb=nb, first=first, last=last, rows_of=rows_of):
            r = t // nb
            j0 = (t % nb) * QB
            qrows = rows_of(r + d * j0, QB)
            krows = rows_of(P + r + d * (j0 - ATT_SIDE), W)
            q = (q_ref[qrows, :] * scale).astype(BF16)
            kslab = kp_s[krows, :].astype(BF16)
            vslab = vp_s[krows, :].astype(BF16)
            lg = lax.dot_general(q, kslab, nt_dims, preferred_element_type=F32) + tb_ref[0, p]
            kpos = col + (j0 - ATT_SIDE)
            lg = jnp.where((kpos >= 0) & (kpos < Sd), lg, NEG_INF)
            m_cur = jnp.max(lg, axis=1, keepdims=True)
            if first:
                m_new = jnp.broadcast_to(m_cur, (QB, LANES))
            else:
                m_old = m_s[qrows, :]
                m_new = jnp.maximum(m_old, m_cur)
            pr = jnp.exp(lg - jnp.concatenate([m_new] * (W // LANES), axis=1))
            l_new = jnp.sum(pr, axis=1, keepdims=True)
            a_new = jnp.dot(pr.astype(BF16), vslab, preferred_element_type=F32)
            if not first:
                alpha = jnp.exp(m_old - m_new)
                l_new = alpha * l_s[qrows, :] + l_new
                a_new = alpha * acc_s[qrows, :] + a_new
            else:
                l_new = jnp.broadcast_to(l_new, (QB, LANES))
            if last:
                acc_s[qrows, :] = a_new / l_new
            else:
                m_s[qrows, :] = m_new
                l_s[qrows, :] = l_new
                acc_s[qrows, :] = a_new
            return carry

        lax.fori_loop(0, d * nb, block, 0)

    o_ref[...] = acc_s[...].astype(o_ref.dtype)


def _t5_bucket(rel):
    nb = T5_BUCKETS // 2
    max_exact = nb // 2
    side = (rel > 0).astype(np.int32) * nb
    n = np.abs(rel)
    large = max_exact + (np.log(np.maximum(n, 1) / max_exact) / np.log(T5_MAX_DISTANCE / max_exact)
                         * (nb - max_exact)).astype(np.int32)
    large = np.minimum(large, nb - 1)
    return side + np.where(n < max_exact, n, large)


def _attn_bias_tables(rel_bias, QB):
    W = QB + 2 * ATT_SIDE
    delta = np.arange(W)[None, :] - np.arange(QB)[:, None]
    band = (delta >= 0) & (delta <= 2 * ATT_SIDE)
    idx = np.stack([_t5_bucket((np.clip(delta, 0, 2 * ATT_SIDE) - ATT_SIDE) * d)
                    for _, d in DIL_PATTERNS])
    tb = jnp.transpose(rel_bias.astype(F32)[idx], (3, 0, 1, 2))
    return jnp.where(band[None, None], tb, NEG_INF)


def dilated_attention(proj, tb, row_start, S, nseq, QB=ATT_QBLOCK):
    rb = row_start // S
    dh = ATT_HEAD_DIM
    W = QB + 2 * ATT_SIDE
    npat = len(DIL_PATTERNS)
    assert S % (DIL_PATTERNS[-1][1] * QB) == 0
    return pl.pallas_call(
        functools.partial(_dilattn_kernel, S=S, QB=QB),
        grid=(nseq, ATT_HEADS),
        in_specs=[
            pl.BlockSpec((S, dh), lambda b, h: (rb + b, COL_AQ // dh + h)),
            pl.BlockSpec((S, dh), lambda b, h: (rb + b, COL_AK // dh + h)),
            pl.BlockSpec((S, dh), lambda b, h: (rb + b, COL_AV // dh + h)),
            pl.BlockSpec((1, npat, QB, W), lambda b, h: (h, 0, 0, 0)),
        ],
        out_specs=pl.BlockSpec((S, dh), lambda b, h: (b, h)),
        out_shape=jax.ShapeDtypeStruct((nseq * S, ATT_WIDTH), BF16),
        scratch_shapes=[pltpu.VMEM((S + 2 * ATT_PAD, dh), F32), pltpu.VMEM((S + 2 * ATT_PAD, dh), F32),
                        pltpu.VMEM((S, dh), F32), pltpu.VMEM((S, LANES), F32), pltpu.VMEM((S, LANES), F32)],
        compiler_params=_params(("parallel", "parallel")),
        name="dilated_attention",
    )(proj, proj, proj, tb)


def _cross_kernel(x_ref, g_ref, wq_ref, kv_ref, wo_ref, o_ref):
    x = x_ref[...]
    h = _rms(x, g_ref[...]).astype(BF16)
    q = jnp.dot(h, wq_ref[...], preferred_element_type=F32).astype(BF16)
    nt_dims = (((1,), (1,)), ((), ()))
    outs = []
    for hd in range(CROSS_HEADS):
        lo = hd * CROSS_HEAD_DIM
        k = kv_ref[0, :, lo:lo + CROSS_HEAD_DIM]
        v = kv_ref[0, :, CROSS_WIDTH + lo:CROSS_WIDTH + lo + CROSS_HEAD_DIM]
        s = lax.dot_general(q[:, lo:lo + CROSS_HEAD_DIM], k, nt_dims,
                            preferred_element_type=F32) * CROSS_HEAD_DIM ** -0.5
        m = jnp.max(s, axis=-1, keepdims=True)
        e = jnp.exp(s - m)
        p = e / jnp.sum(e, axis=-1, keepdims=True)
        outs.append(jnp.dot(p.astype(BF16), v, preferred_element_type=F32).astype(BF16))
    o = jnp.concatenate(outs, axis=1)
    o_ref[...] = x + jnp.dot(o, wo_ref[...], preferred_element_type=F32)


def cross_attention(x, g, wq, kv, wo, seq_of_tile, ts):
    M, D = x.shape
    Mm = kv.shape[1]
    assert M % ts == 0, (M, ts)
    return pl.pallas_call(
        _cross_kernel,
        grid=(M // ts,),
        in_specs=[
            pl.BlockSpec((ts, D), lambda i: (i, 0)),
            pl.BlockSpec((1, D), lambda i: (0, 0)),
            pl.BlockSpec((D, CROSS_WIDTH), lambda i: (0, 0)),
            pl.BlockSpec((1, Mm, 2 * CROSS_WIDTH), lambda i: (seq_of_tile(i), 0, 0)),
            pl.BlockSpec((CROSS_WIDTH, D), lambda i: (0, 0)),
        ],
        out_specs=pl.BlockSpec((ts, D), lambda i: (i, 0)),
        out_shape=jax.ShapeDtypeStruct((M, D), F32),
        compiler_params=_params(("parallel",)),
        name="cross_attention",
    )(x, g.reshape(1, D), wq, kv, wo)


def _gateup_kernel(be_ref, na_ref, x_ref, wg_ref, wu_ref, o_ref):
    i = pl.program_id(0)

    @pl.when(i < na_ref[0])
    def _():
        x = x_ref[...]
        g = jnp.dot(x, wg_ref[0], preferred_element_type=F32)
        u = jnp.dot(x, wu_ref[0], preferred_element_type=F32)
        o_ref[...] = (g * jax.nn.sigmoid(g) * u).astype(o_ref.dtype)

    @pl.when(i >= na_ref[0])
    def _():
        o_ref[...] = jnp.zeros_like(o_ref)


def gateup(x, wg, wu, block_expert, n_active, tm, tf=512):
    M, D = x.shape
    F = wg.shape[2]
    assert M % tm == 0 and F % tf == 0, (M, tm, F, tf)
    grid_spec = pltpu.PrefetchScalarGridSpec(
        num_scalar_prefetch=2,
        grid=(M // tm, F // tf),
        in_specs=[
            pl.BlockSpec((tm, D), lambda i, j, be, na: (i, 0)),
            pl.BlockSpec((1, D, tf), lambda i, j, be, na: (be[i], 0, j)),
            pl.BlockSpec((1, D, tf), lambda i, j, be, na: (be[i], 0, j)),
        ],
        out_specs=pl.BlockSpec((tm, tf), lambda i, j, be, na: (i, j)),
    )
    return pl.pallas_call(
        _gateup_kernel,
        grid_spec=grid_spec,
        out_shape=jax.ShapeDtypeStruct((M, F), BF16),
        compiler_params=_params(("parallel", "parallel")),
        name="ffn_gateup",
    )(block_expert, n_active, x, wg, wu)


def _down_kernel(be_ref, na_ref, h_ref, wd_ref, *rest, has_res):
    o_ref = rest[-1]
    i, k = pl.program_id(0), pl.program_id(1)

    @pl.when(k == 0)
    def _():
        o_ref[...] = rest[0][...] if has_res else jnp.zeros_like(o_ref)

    @pl.when(i < na_ref[0])
    def _():
        o_ref[...] += jnp.dot(h_ref[...], wd_ref[0], preferred_element_type=F32)


def down(h, wd, block_expert, n_active, tm, res=None, tk=1024):
    M, F = h.shape
    D = wd.shape[2]
    assert M % tm == 0 and F % tk == 0, (M, tm, F, tk)
    in_specs = [
        pl.BlockSpec((tm, tk), lambda i, k, be, na: (i, k)),
        pl.BlockSpec((1, tk, D), lambda i, k, be, na: (be[i], k, 0)),
    ]
    args = [h, wd]
    if res is not None:
        in_specs.append(pl.BlockSpec((tm, D), lambda i, k, be, na: (i, 0)))
        args.append(res)
    grid_spec = pltpu.PrefetchScalarGridSpec(
        num_scalar_prefetch=2,
        grid=(M // tm, F // tk),
        in_specs=in_specs,
        out_specs=pl.BlockSpec((tm, D), lambda i, k, be, na: (i, 0)),
    )
    return pl.pallas_call(
        functools.partial(_down_kernel, has_res=res is not None),
        grid_spec=grid_spec,
        out_shape=jax.ShapeDtypeStruct((M, D), F32),
        compiler_params=_params(("parallel", "arbitrary")),
        name="ffn_down",
    )(block_expert, n_active, *args)


def _router_kernel(x_ref, g_ref, wr_ref, o_ref):
    h = _rms(x_ref[...], g_ref[...])
    logits = jnp.dot(h, wr_ref[...], preferred_element_type=F32, precision=lax.Precision.HIGHEST)
    lane = lax.broadcasted_iota(jnp.int32, logits.shape, 1).astype(F32)
    lg = jnp.where(lane < N_EXPERTS, logits, -jnp.inf)
    m1 = jnp.max(lg, axis=1, keepdims=True)
    i1 = jnp.min(jnp.where(lg == m1, lane, float(LANES)), axis=1, keepdims=True)
    lg2 = jnp.where(lane == i1, -jnp.inf, lg)
    m2 = jnp.max(lg2, axis=1, keepdims=True)
    i2 = jnp.min(jnp.where(lg2 == m2, lane, float(LANES)), axis=1, keepdims=True)
    e = jnp.exp(m2 - m1)
    den = 1.0 + e
    out = jnp.where(lane == 0, 1.0 / den, jnp.where(lane == 1, e / den, jnp.where(lane == 2, i1, i2)))
    o_ref[...] = out


def router(x, g, w_router, tm=512):
    M, D = x.shape
    assert M % tm == 0, (M, tm)
    wr = jnp.zeros((D, LANES), F32).at[:, :N_EXPERTS].set(w_router)
    out = pl.pallas_call(
        _router_kernel,
        grid=(M // tm,),
        in_specs=[pl.BlockSpec((tm, D), lambda i: (i, 0)),
                  pl.BlockSpec((1, D), lambda i: (0, 0)),
                  pl.BlockSpec((D, LANES), lambda i: (0, 0))],
        out_specs=pl.BlockSpec((tm, LANES), lambda i: (i, 0)),
        out_shape=jax.ShapeDtypeStruct((M, LANES), F32),
        compiler_params=_params(("parallel",)),
        name="moe_router",
    )(x, g.reshape(1, D), wr)
    return out[:, 0:TOP_K], out[:, TOP_K:2 * TOP_K].astype(jnp.int32)


def _row_copy(src_hbm, dst_vmem, sem, src_row, dst_row):
    return pltpu.make_async_copy(src_hbm.at[pl.ds(src_row, 1), :], dst_vmem.at[pl.ds(dst_row, 1), :], sem)


def _dispatch_kernel(tok_ref, x_hbm, g_ref, o_ref, buf, sem, *, tg):
    base = pl.program_id(0) * tg

    def start(r, c):
        _row_copy(x_hbm, buf, sem, tok_ref[base + r], r).start()
        return c

    lax.fori_loop(0, tg, start, 0)

    def wait(r, c):
        _row_copy(x_hbm, buf, sem, 0, r).wait()
        return c

    lax.fori_loop(0, tg, wait, 0)
    o_ref[...] = _rms(buf[...], g_ref[...]).astype(o_ref.dtype)


def moe_dispatch(x, g, slot_tok, tg=256):
    D = x.shape[1]
    n_slots = slot_tok.shape[0]
    assert n_slots % tg == 0, (n_slots, tg)
    grid_spec = pltpu.PrefetchScalarGridSpec(
        num_scalar_prefetch=1,
        grid=(n_slots // tg,),
        in_specs=[pl.BlockSpec(memory_space=pl.ANY),
                  pl.BlockSpec((1, D), lambda i, tok: (0, 0))],
        out_specs=pl.BlockSpec((tg, D), lambda i, tok: (i, 0)),
        scratch_shapes=[pltpu.VMEM((tg, D), F32), pltpu.SemaphoreType.DMA(())],
    )
    return pl.pallas_call(
        functools.partial(_dispatch_kernel, tg=tg),
        grid_spec=grid_spec,
        out_shape=jax.ShapeDtypeStruct((n_slots, D), BF16),
        compiler_params=_params(("arbitrary",)),
        name="moe_dispatch",
    )(slot_tok, x, g.reshape(1, D))


def _combine_kernel(slot_ref, x_ref, gate_ref, ys_hbm, o_ref, buf, sem, *, tc):
    base = pl.program_id(0) * tc * TOP_K

    def start(r, c):
        for k in range(TOP_K):
            _row_copy(ys_hbm, buf.at[k], sem.at[k], slot_ref[base + TOP_K * r + k], r).start()
        return c

    lax.fori_loop(0, tc, start, 0)

    def wait(r, c):
        for k in range(TOP_K):
            _row_copy(ys_hbm, buf.at[k], sem.at[k], 0, r).wait()
        return c

    lax.fori_loop(0, tc, wait, 0)
    gate = gate_ref[...]
    y = x_ref[...]
    for k in range(TOP_K):
        y = y + buf[k] * gate[:, k:k + 1]
    o_ref[...] = y


def moe_combine(x, gates, slots, ys, tc=256):
    M, D = x.shape
    assert M % tc == 0, (M, tc)
    grid_spec = pltpu.PrefetchScalarGridSpec(
        num_scalar_prefetch=1,
        grid=(M // tc,),
        in_specs=[pl.BlockSpec((tc, D), lambda i, s: (i, 0)),
                  pl.BlockSpec((tc, TOP_K), lambda i, s: (i, 0)),
                  pl.BlockSpec(memory_space=pl.ANY)],
        out_specs=pl.BlockSpec((tc, D), lambda i, s: (i, 0)),
        scratch_shapes=[pltpu.VMEM((TOP_K, tc, D), F32), pltpu.SemaphoreType.DMA((TOP_K,))],
    )
    return pl.pallas_call(
        functools.partial(_combine_kernel, tc=tc),
        grid_spec=grid_spec,
        out_shape=jax.ShapeDtypeStruct((M, D), F32),
        compiler_params=_params(("arbitrary",)),
        name="moe_combine",
    )(slots, x, gates, ys)


def _moe_plan(idx, tm):
    N = idx.shape[0]
    A = N * TOP_K
    n_blocks = A // tm + N_EXPERTS
    flat_e = idx.reshape(A)
    onehot = (flat_e[:, None] == jnp.arange(N_EXPERTS, dtype=jnp.int32)[None, :]).astype(jnp.int32)
    csum = jnp.cumsum(onehot, axis=0)
    counts = csum[-1]
    rank = jnp.sum(csum * onehot, axis=1) - 1
    padded = (counts + tm - 1) // tm * tm
    pad_end = jnp.cumsum(padded)
    slot = ((pad_end - padded)[flat_e] + rank).astype(jnp.int32)
    slot_tok = jnp.zeros((n_blocks * tm,), jnp.int32).at[slot].set(jnp.arange(A, dtype=jnp.int32) // TOP_K)
    starts = jnp.arange(n_blocks, dtype=jnp.int32) * tm
    block_expert = jnp.minimum(jnp.sum(starts[:, None] >= pad_end[None, :], axis=1), N_EXPERTS - 1)
    n_active = (pad_end[-1] // tm).astype(jnp.int32).reshape(1)
    return slot, slot_tok, block_expert.astype(jnp.int32), n_active


def moe_block(x, g, w_router, wg, wu, wd):
    gates, idx = router(x, g, w_router)
    slot, slot_tok, block_expert, n_active = _moe_plan(idx, MOE_TM)
    xs = moe_dispatch(x, g, slot_tok)
    hid = gateup(xs, wg, wu, block_expert, n_active, MOE_TM)
    ys = down(hid, wd, block_expert, n_active, MOE_TM)
    return moe_combine(x, gates, slot, ys)


def dense_block(x, g, wg, wu, wd, tm_up=1024, tm_down=512):
    M = x.shape[0]
    be = jnp.zeros((M // tm_down,), jnp.int32)
    na = jnp.full((1,), M // tm_down, jnp.int32)
    h = rmsnorm(x, g, BF16)
    hid = gateup(h, wg, wu, be, na, tm_up)
    return down(hid, wd, be, na, tm_down, res=x)


def _pad_to(w, axis, mult):
    n = w.shape[axis]
    pad = (-n) % mult
    if pad == 0:
        return w
    widths = [(0, 0)] * w.ndim
    widths[axis] = (0, pad)
    return jnp.pad(w, widths)


def kernel(x_prompt, x_sample, mem_prompt, mem_sample, rel_bias, g_mix, w_in, w_out, g_cross, g_mem, w_cq, w_ckv, w_co, g_ffn, w_ff_gate, w_ff_up, w_ff_down, w_router, w_exp_gate, w_exp_up, w_exp_down, g_final):
    Bp, Sp, D = x_prompt.shape
    Bs, Ss, _ = x_sample.shape
    Mm = mem_prompt.shape[1]
    groups = ((0, Sp, Bp), (Bp * Sp, Ss, Bs))
    x = jnp.concatenate([x_prompt.reshape(Bp * Sp, D), x_sample.reshape(Bs * Ss, D)], axis=0)
    mem = jnp.concatenate([mem_prompt.reshape(Bp * Mm, D), mem_sample.reshape(Bs * Mm, D)], axis=0)

    ts = 512

    def seq_of_tile(i):
        tiles_p = Bp * Sp // ts
        return jnp.where(i < tiles_p, i // (Sp // ts), Bp + (i - tiles_p) // (Ss // ts))

    tb = _attn_bias_tables(rel_bias, ATT_QBLOCK)
    bf = lambda w: w.astype(BF16)

    for l in range(DEPTH):
        h = rmsnorm(x, g_mix[l], BF16)
        proj = matmul(h, bf(w_in[l]), out_dtype=F32)
        ret = jnp.concatenate([retention(proj, r0, S, n) for r0, S, n in groups], axis=0)
        att = jnp.concatenate([dilated_attention(proj, tb, r0, S, n) for r0, S, n in groups], axis=0)
        x = matmul(jnp.concatenate([ret, att], axis=1), bf(w_out[l]), res=x, out_dtype=F32)
        kv = matmul(rmsnorm(mem, g_mem[l], BF16, tm=Mm), bf(w_ckv[l]), out_dtype=BF16, tm=2 * Mm)
        x = cross_attention(x, g_cross[l], bf(w_cq[l]), kv.reshape(Bp + Bs, Mm, 2 * CROSS_WIDTH),
                            bf(w_co[l]), seq_of_tile, ts)
        i = l // 2
        if l % 2 == 0:
            x = dense_block(x, g_ffn[l],
                            bf(_pad_to(w_ff_gate[i], 1, FF_PAD))[None],
                            bf(_pad_to(w_ff_up[i], 1, FF_PAD))[None],
                            bf(_pad_to(w_ff_down[i], 0, FF_PAD))[None])
        else:
            x = moe_block(x, g_ffn[l], w_router[i], bf(w_exp_gate[i]), bf(w_exp_up[i]), bf(w_exp_down[i]))

    y = rmsnorm(x, g_final, F32)
    n_p = Bp * Sp
    return (y[:n_p].reshape(Bp, Sp, D), y[n_p:].reshape(Bs, Ss, D))
```
